```python
import math
import jax, jax.numpy as jnp
from jax import lax
import numpy as np

D_MODEL = 1024
BATCH = 4
SEQ = 8192
DEPTH = 2
DEC_BATCH = 128
DEC_SEQ = 4
PAST_LEN = 16384
PAGE_SIZE = 128

GLA_HEADS = 4
GLA_DK = 64
GLA_DV = 128
GLA_GATE_RANK = 16
GLA_TAU = 16.0
GLA_CHUNK = 16
SB_HEADS = 8
SB_KV_HEADS = 2
SB_HEAD_DIM = 64
SB_SCALE = SB_HEAD_DIM ** -0.5
MLA_HEADS = 8
MLA_Q_RANK = 256
MLA_KV_RANK = 128
MLA_NOPE = 64
MLA_ROPE = 32
MLA_V = 64
MLA_SCALE = (MLA_NOPE + MLA_ROPE) ** -0.5
ROPE_THETA = 10000.0
D_FF = 2816
Q_BLOCK = 128
N_BRANCH = 3
N_MOD = 9
EPS = 1e-6
DEEPNORM_ALPHA = (2 * DEPTH) ** 0.25
DEEPNORM_BETA = (8 * DEPTH) ** -0.25
SPLIT_SIZES = (GLA_HEADS * GLA_DK, GLA_HEADS * GLA_DK, GLA_HEADS * GLA_DV, GLA_HEADS * GLA_DV,
               GLA_GATE_RANK,
               SB_HEADS * SB_HEAD_DIM, SB_KV_HEADS * SB_HEAD_DIM, SB_KV_HEADS * SB_HEAD_DIM,
               MLA_Q_RANK, MLA_KV_RANK, MLA_ROPE,
               N_BRANCH * D_MODEL)
N_IN = sum(SPLIT_SIZES)

kernel_name = 'hybrid_gla_stickbreak_mla_decoder_step'


def rms_norm(x, g):
    xf = x.astype(jnp.float32)
    y = xf * lax.rsqrt(jnp.mean(jnp.square(xf), axis=-1, keepdims=True) + EPS)
    return (y * g).astype(x.dtype)


def layer_norm(x, g, b):
    xf = x.astype(jnp.float32)
    mu = jnp.mean(xf, axis=-1, keepdims=True)
    var = jnp.mean(jnp.square(xf - mu), axis=-1, keepdims=True)
    return ((xf - mu) * lax.rsqrt(var + EPS) * g + b).astype(x.dtype)


def rope(x, pos):
    half = x.shape[-1] // 2
    inv_freq = ROPE_THETA ** (-jnp.arange(half, dtype=jnp.float32) / half)
    ang = pos.astype(jnp.float32)[:, None] * inv_freq[None, :]
    shape = (1, pos.shape[0]) + (1,) * (x.ndim - 3) + (half,)
    cos, sin = jnp.cos(ang).reshape(shape), jnp.sin(ang).reshape(shape)
    x1 = x[..., :half].astype(jnp.float32)
    x2 = x[..., half:].astype(jnp.float32)
    return jnp.concatenate([x1 * cos - x2 * sin, x1 * sin + x2 * cos], axis=-1).astype(x.dtype)


def swiglu(h, w_up, w_down):
    a, b = jnp.split(h @ w_up, 2, axis=-1)
    return (jax.nn.silu(a) * b) @ w_down


def gla_chunked(q, k, v, log_a, s0):
    B, L, H, DK = q.shape
    C = math.gcd(L, GLA_CHUNK)
    N = L // C
    f32 = jnp.float32
    q, k, la = (t.astype(f32).reshape(B, N, C, H, DK) for t in (q, k, log_a))
    v = v.astype(f32).reshape(B, N, C, H, -1)
    b = jnp.cumsum(la, axis=2)
    causal = jnp.tril(jnp.ones((C, C), bool))[:, :, None, None]
    rel = jnp.where(causal, b[:, :, :, None] - b[:, :, None, :], -jnp.inf)
    attn = jnp.einsum('bnthd,bnshd,bntshd->bnhts', q, k, jnp.exp(rel))
    o_intra = jnp.einsum('bnhts,bnshv->bnthv', attn, v)
    q_in = q * jnp.exp(b)
    b_last = b[:, :, -1]
    k_out = k * jnp.exp(b_last[:, :, None] - b)
    u = jnp.einsum('bnshd,bnshv->bnhdv', k_out, v)

    def step(s, inp):
        qi, gi, ui = inp
        o = jnp.einsum('bthd,bhdv->bthv', qi, s)
        return jnp.exp(gi)[..., None] * s + ui, o

    s_fin, o_inter = lax.scan(step, s0.astype(f32),
                              (jnp.moveaxis(q_in, 1, 0), jnp.moveaxis(b_last, 1, 0), jnp.moveaxis(u, 1, 0)))
    o = o_intra + jnp.moveaxis(o_inter, 0, 1)
    return o.reshape(B, L, H, -1), s_fin


def sb_attend(q, k, v, q_pos, k_pos):
    z = jnp.einsum('btkgd,bskd->bkgts', q, k, preferred_element_type=jnp.float32) * SB_SCALE
    earlier = k_pos[None, :] < q_pos[:, None]
    log_keep = jnp.where(earlier, jax.nn.log_sigmoid(-z), 0.0)
    log_between = lax.cumsum(log_keep, axis=z.ndim - 1, reverse=True) - log_keep
    w = jnp.where(earlier, jnp.exp(jax.nn.log_sigmoid(z) + log_between), 0.0)
    return jnp.einsum('bkgts,bskd->btkgd', w.astype(v.dtype), v)


def mla_attend(q_lat, q_rope, ckv, kr, q_pos, k_pos):
    s = (jnp.einsum('bthr,bsr->bhts', q_lat, ckv, preferred_element_type=jnp.float32)
         + jnp.einsum('bthe,bse->bhts', q_rope, kr, preferred_element_type=jnp.float32)) * MLA_SCALE
    s = jnp.where(k_pos[None, :] <= q_pos[:, None], s, -jnp.inf)
    p = jax.nn.softmax(s, axis=-1)
    return jnp.einsum('bhts,bsr->bthr', p.astype(ckv.dtype), ckv)


def over_query_blocks(fn, qs, q_pos):
    n_blk = q_pos.shape[0] // Q_BLOCK

    def to_blocks(t):
        return jnp.moveaxis(t.reshape((t.shape[0], n_blk, Q_BLOCK) + t.shape[2:]), 1, 0)

    out = lax.map(lambda a: fn(*a[0], a[1]),
                  (tuple(to_blocks(t) for t in qs), q_pos.reshape(n_blk, Q_BLOCK)))
    out = jnp.moveaxis(out, 0, 1)
    return out.reshape((out.shape[0], n_blk * Q_BLOCK) + out.shape[3:])


def gather_pages(cache, page_table, layer):
    rows = cache[page_table, layer]
    return rows.reshape((rows.shape[0], rows.shape[1] * rows.shape[2]) + rows.shape[3:])


def mixer_inputs(h, pos, lp):
    B, L, _ = h.shape
    offs = np.cumsum(SPLIT_SIZES)[:-1].tolist()
    gq, gk, gv, gr, gg, sq, sk, sv, dq, dkv, dkr, gates = jnp.split(h @ lp['w_in'], offs, axis=-1)
    gla_la = jax.nn.log_sigmoid((gg @ lp['w_gla_gate'] + lp['b_gla_gate']).astype(jnp.float32)) / GLA_TAU
    cq = rms_norm(dq, lp['g_q_norm'])
    q = (cq @ lp['w_uq']).reshape(B, L, MLA_HEADS, MLA_NOPE + MLA_ROPE)
    w_uk = lp['w_uk'].reshape(MLA_KV_RANK, MLA_HEADS, MLA_NOPE)
    return dict(
        gla_q=gq.reshape(B, L, GLA_HEADS, GLA_DK) * GLA_DK ** -0.5,
        gla_k=gk.reshape(B, L, GLA_HEADS, GLA_DK),
        gla_v=gv.reshape(B, L, GLA_HEADS, GLA_DV),
        gla_r=gr.reshape(B, L, GLA_HEADS, GLA_DV),
        gla_la=gla_la.reshape(B, L, GLA_HEADS, GLA_DK),
        sb_q=sq.reshape(B, L, SB_KV_HEADS, SB_HEADS // SB_KV_HEADS, SB_HEAD_DIM),
        sb_k=sk.reshape(B, L, SB_KV_HEADS, SB_HEAD_DIM),
        sb_v=sv.reshape(B, L, SB_KV_HEADS, SB_HEAD_DIM),
        q_lat=jnp.einsum('blhn,rhn->blhr', q[..., :MLA_NOPE], w_uk),
        q_rope=rope(q[..., MLA_NOPE:], pos),
        ckv=rms_norm(dkv, lp['g_kv_norm']),
        kr=rope(dkr, pos),
        gates=gates,
    )


def mixer_merge(h, mi, o_gla, o_sb, o_mla_lat, lp):
    B, L, _ = h.shape
    og = rms_norm(o_gla, lp['g_gla_norm'].reshape(GLA_HEADS, GLA_DV)) * jax.nn.silu(mi['gla_r'].astype(jnp.float32))
    br_gla = og.reshape(B, L, -1).astype(h.dtype) @ lp['w_br_gla']
    br_sb = o_sb.reshape(B, L, -1).astype(h.dtype) @ lp['w_br_sb']
    w_uv = lp['w_uv'].reshape(MLA_KV_RANK, MLA_HEADS, MLA_V)
    v_mla = jnp.einsum('blhr,rhv->blhv', o_mla_lat, w_uv).reshape(B, L, -1)
    br_mla = v_mla.astype(h.dtype) @ lp['w_br_mla']
    g = jax.nn.sigmoid(mi['gates'].astype(jnp.float32)).reshape(B, L, N_BRANCH, D_MODEL)
    merged = g[:, :, 0] * br_gla + g[:, :, 1] * br_sb + g[:, :, 2] * br_mla
    return merged.astype(h.dtype) @ lp['w_out']


def prompt_mixer(h, lp):
    B, L, _ = h.shape
    pos = jnp.arange(L)
    mi = mixer_inputs(h, pos, lp)
    o_gla, s_gla = gla_chunked(mi['gla_q'], mi['gla_k'], mi['gla_v'], mi['gla_la'],
                               jnp.zeros((B, GLA_HEADS, GLA_DK, GLA_DV), jnp.float32))
    o_sb = over_query_blocks(lambda q, qp: sb_attend(q, mi['sb_k'], mi['sb_v'], qp, pos), (mi['sb_q'],), pos)
    o_mla = over_query_blocks(lambda ql, qr, qp: mla_attend(ql, qr, mi['ckv'], mi['kr'], qp, pos),
                              (mi['q_lat'], mi['q_rope']), pos)
    out = mixer_merge(h, mi, o_gla, o_sb, o_mla, lp)
    return out, (mi['sb_k'], mi['sb_v'], mi['ckv'], mi['kr'], s_gla)


def sample_mixer(h, lp, past_sb_k, past_sb_v, past_ckv, past_kr, gla_state):
    L = h.shape[1]
    past_len = past_ckv.shape[1]
    q_pos = past_len + jnp.arange(L)
    k_pos = jnp.arange(past_len + L)
    mi = mixer_inputs(h, q_pos, lp)
    o_gla, s_gla = gla_chunked(mi['gla_q'], mi['gla_k'], mi['gla_v'], mi['gla_la'], gla_state)

    def cat(past, new):
        return jnp.concatenate([past, new.astype(past.dtype)], axis=1)

    o_sb = sb_attend(mi['sb_q'], cat(past_sb_k, mi['sb_k']), cat(past_sb_v, mi['sb_v']), q_pos, k_pos)
    o_mla = mla_attend(mi['q_lat'], mi['q_rope'], cat(past_ckv, mi['ckv']), cat(past_kr, mi['kr']), q_pos, k_pos)
    out = mixer_merge(h, mi, o_gla, o_sb, o_mla, lp)
    return out, (mi['sb_k'], mi['sb_v'], mi['ckv'], mi['kr'], s_gla)


def trunk_layer(x, c, lp, mixer):
    mod = (jax.nn.silu(c) @ lp['w_mod'] + lp['b_mod']).reshape(c.shape[0], 1, N_MOD, D_MODEL)
    m = lambda i: mod[:, :, i]
    h = x * (1 + m(1)) + m(0)
    x = layer_norm(DEEPNORM_ALPHA * x + 0.5 * m(2) * swiglu(h, lp['w_ffn_up'][0], lp['w_ffn_down'][0]),
                   lp['ln_g'][0], lp['ln_b'][0])
    h = x * (1 + m(4)) + m(3)
    mix, state = mixer(h)
    x = layer_norm(DEEPNORM_ALPHA * x + m(5) * mix, lp['ln_g'][1], lp['ln_b'][1])
    h = x * (1 + m(7)) + m(6)
    x = layer_norm(DEEPNORM_ALPHA * x + 0.5 * m(8) * swiglu(h, lp['w_ffn_up'][1], lp['w_ffn_down'][1]),
                   lp['ln_g'][2], lp['ln_b'][2])
    return x, state


def setup_inputs(seed: int = 0) -> dict:
    key = jax.random.key(seed)
    ks = iter(jax.random.split(key, 40))

    def nrm(shape, scale=1.0):
        return jax.random.normal(next(ks), shape, jnp.float32) * scale

    n_pages = PAST_LEN // PAGE_SIZE
    n_used = DEC_BATCH * n_pages
    n_pool = n_used + n_used // 4
    page_table = jax.random.permutation(next(ks), n_pool)[:n_used].reshape(DEC_BATCH, n_pages).astype(jnp.int32)
    gla_w = GLA_HEADS * GLA_DV
    sb_w = SB_HEADS * SB_HEAD_DIM
    mla_w = MLA_HEADS * MLA_V
    return {
        'x_prompt': nrm((BATCH, SEQ, D_MODEL)),
        'x_sample': nrm((DEC_BATCH, DEC_SEQ, D_MODEL)),
        'cache_sb_k': nrm((n_pool, DEPTH, PAGE_SIZE, SB_KV_HEADS, SB_HEAD_DIM)),
        'cache_sb_v': nrm((n_pool, DEPTH, PAGE_SIZE, SB_KV_HEADS, SB_HEAD_DIM)),
        'cache_mla_ckv': nrm((n_pool, DEPTH, PAGE_SIZE, MLA_KV_RANK)),
        'cache_mla_kr': nrm((n_pool, DEPTH, PAGE_SIZE, MLA_ROPE)),
        'state_gla': nrm((DEC_BATCH, DEPTH, GLA_HEADS, GLA_DK, GLA_DV)),
        'page_table': page_table,
        'c_prompt': nrm((BATCH, D_MODEL)),
        'c_sample': nrm((DEC_BATCH, D_MODEL)),
        'w_in': nrm((DEPTH, D_MODEL, N_IN), D_MODEL ** -0.5),
        'w_gla_gate': nrm((DEPTH, GLA_GATE_RANK, GLA_HEADS * GLA_DK), GLA_GATE_RANK ** -0.5),
        'b_gla_gate': nrm((DEPTH, GLA_HEADS * GLA_DK), 0.1),
        'g_gla_norm': 1.0 + nrm((DEPTH, gla_w), 0.02),
        'g_q_norm': 1.0 + nrm((DEPTH, MLA_Q_RANK), 0.02),
        'w_uq': nrm((DEPTH, MLA_Q_RANK, MLA_HEADS * (MLA_NOPE + MLA_ROPE)), MLA_Q_RANK ** -0.5),
        'g_kv_norm': 1.0 + nrm((DEPTH, MLA_KV_RANK), 0.02),
        'w_uk': nrm((DEPTH, MLA_KV_RANK, MLA_HEADS * MLA_NOPE), MLA_KV_RANK ** -0.5),
        'w_uv': nrm((DEPTH, MLA_KV_RANK, MLA_HEADS * MLA_V), MLA_KV_RANK ** -0.5),
        'w_br_gla': nrm((DEPTH, gla_w, D_MODEL), gla_w ** -0.5 * DEEPNORM_BETA),
        'w_br_sb': nrm((DEPTH, sb_w, D_MODEL), sb_w ** -0.5 * DEEPNORM_BETA),
        'w_br_mla': nrm((DEPTH, mla_w, D_MODEL), mla_w ** -0.5 * DEEPNORM_BETA),
        'w_out': nrm((DEPTH, D_MODEL, D_MODEL), D_MODEL ** -0.5 * DEEPNORM_BETA),
        'w_mod': nrm((DEPTH, D_MODEL, N_MOD * D_MODEL), D_MODEL ** -0.5),
        'b_mod': nrm((DEPTH, N_MOD * D_MODEL), 0.1),
        'ln_g': 1.0 + nrm((DEPTH, 3, D_MODEL), 0.02),
        'ln_b': nrm((DEPTH, 3, D_MODEL), 0.02),
        'w_ffn_up': nrm((DEPTH, 2, D_MODEL, 2 * D_FF), D_MODEL ** -0.5),
        'w_ffn_down': nrm((DEPTH, 2, D_FF, D_MODEL), D_FF ** -0.5 * DEEPNORM_BETA),
    }


def reference(x_prompt, x_sample, cache_sb_k, cache_sb_v, cache_mla_ckv, cache_mla_kr, state_gla,
              page_table, c_prompt, c_sample, w_in, w_gla_gate, b_gla_gate, g_gla_norm, g_q_norm, w_uq,
              g_kv_norm, w_uk, w_uv, w_br_gla, w_br_sb, w_br_mla, w_out, w_mod, b_mod, ln_g, ln_b,
              w_ffn_up, w_ffn_down):
    yp, ys = x_prompt, x_sample
    states_p, states_s = [], []
    for l in range(DEPTH):
        lp = dict(w_in=w_in[l], w_gla_gate=w_gla_gate[l], b_gla_gate=b_gla_gate[l], g_gla_norm=g_gla_norm[l],
                  g_q_norm=g_q_norm[l], w_uq=w_uq[l], g_kv_norm=g_kv_norm[l], w_uk=w_uk[l], w_uv=w_uv[l],
                  w_br_gla=w_br_gla[l], w_br_sb=w_br_sb[l], w_br_mla=w_br_mla[l], w_out=w_out[l],
                  w_mod=w_mod[l], b_mod=b_mod[l], ln_g=ln_g[l], ln_b=ln_b[l],
                  w_ffn_up=w_ffn_up[l], w_ffn_down=w_ffn_down[l])
        yp, st_p = trunk_layer(yp, c_prompt, lp, lambda h: prompt_mixer(h, lp))
        past = (gather_pages(cache_sb_k, page_table, l), gather_pages(cache_sb_v, page_table, l),
                gather_pages(cache_mla_ckv, page_table, l), gather_pages(cache_mla_kr, page_table, l))
        ys, st_s = trunk_layer(ys, c_sample, lp, lambda h: sample_mixer(h, lp, *past, state_gla[:, l]))
        states_p.append(st_p)
        states_s.append(st_s)
    p_sb_k, p_sb_v, p_ckv, p_kr, p_gla = (jnp.stack([s[i] for s in states_p], axis=1) for i in range(5))
    s_sb_k, s_sb_v, s_ckv, s_kr, s_gla = (jnp.stack([s[i] for s in states_s], axis=1) for i in range(5))
    return (yp, ys, p_sb_k, p_sb_v, p_ckv, p_kr, p_gla, s_sb_k, s_sb_v, s_ckv, s_kr, s_gla)
```

```python
import functools

import numpy as np
import jax
import jax.numpy as jnp
from jax import lax
from jax.experimental import pallas as pl
from jax.experimental.pallas import tpu as pltpu

F32 = jnp.float32
BF16 = jnp.bfloat16

D_MODEL = 1024
GLA_HEADS, GLA_DK, GLA_DV, GLA_GATE_RANK, GLA_TAU = 4, 64, 128, 16, 16.0
SB_HEADS, SB_KV_HEADS, SB_HEAD_DIM = 8, 2, 64
SB_SCALE = SB_HEAD_DIM ** -0.5
MLA_HEADS, MLA_Q_RANK, MLA_KV_RANK, MLA_NOPE, MLA_ROPE, MLA_V = 8, 256, 128, 64, 32, 64
MLA_SCALE = (MLA_NOPE + MLA_ROPE) ** -0.5
ROPE_THETA = 10000.0
D_FF = 2816
N_MOD = 9
EPS = 1e-6
PAGE = 128
LANES = 128
VMEM_LIMIT = 56 * 1024 * 1024
NEG_BIG = -1e30
SB_DEAD = -90.0

C_GQ, C_GK, C_GV, C_GR = 0, 256, 512, 1024
C_SQ = 1536
C_SK, C_SV = 2560, 2688
C_DQ, C_DKV = 2816, 3072
C_KR, C_KRS, C_GG = 3200, 3328, 3456
N_INRE = 3584


def _cparams(sem):
    return pltpu.CompilerParams(dimension_semantics=sem, vmem_limit_bytes=VMEM_LIMIT)


def _dot(a, b):
    return jnp.dot(a, b, preferred_element_type=F32)


def _dot_nt(a, b):
    return lax.dot_general(a, b, (((1,), (1,)), ((), ())), preferred_element_type=F32)


def _dot_tn(a, b, precision=None):
    return lax.dot_general(a, b, (((0,), (0,)), ((), ())), preferred_element_type=F32,
                           precision=precision)


def _layer_norm(y, g, b):
    mu = jnp.mean(y, axis=-1, keepdims=True)
    d = y - mu
    var = jnp.mean(d * d, axis=-1, keepdims=True)
    return d * lax.rsqrt(var + EPS) * g + b


def _rms(x, g):
    return x * lax.rsqrt(jnp.mean(x * x, axis=-1, keepdims=True) + EPS) * g


def _const_spec(shape):
    nd = len(shape)
    return pl.BlockSpec(shape, lambda *_: (0,) * nd)


def _mod_kernel(c_ref, w_ref, b_ref, o_ref):
    c = c_ref[...]
    a = (c * jax.nn.sigmoid(c)).astype(BF16)
    o_ref[...] = _dot(a, w_ref[...]) + b_ref[...]


def _modulation(c, w_mod, b_mod):
    g, n = c.shape[0], w_mod.shape[1]
    tn = 1536
    return pl.pallas_call(
        _mod_kernel,
        grid=(n // tn,),
        in_specs=[pl.BlockSpec((g, D_MODEL), lambda j: (0, 0)),
                  pl.BlockSpec((D_MODEL, tn), lambda j: (0, j)),
                  pl.BlockSpec((1, tn), lambda j: (0, j))],
        out_specs=pl.BlockSpec((g, tn), lambda j: (0, j)),
        out_shape=jax.ShapeDtypeStruct((g, n), F32),
        compiler_params=_cparams(("arbitrary",)),
        name="modulation",
    )(c, w_mod, b_mod.reshape(1, n))


class _Mod:
    def __init__(self, mod, per_token, tiles_per_seq):
        self.per_token = per_token
        self.tiles_per_seq = tiles_per_seq
        self.arr = mod if per_token else mod.reshape(mod.shape[0], N_MOD, 1, D_MODEL)

    def spec(self, j, tm):
        if self.per_token:
            return pl.BlockSpec((tm, D_MODEL), lambda i: (i, j))
        tps = self.tiles_per_seq
        return pl.BlockSpec((None, None, 1, D_MODEL), lambda i: (i // tps, j, 0, 0))


def _ffn_kernel(x_ref, sh_ref, sc_ref, gt_ref, wu_ref, wd_ref, lg_ref, lb_ref, o_ref, acc_ref,
                *, alpha, tf):
    x = x_ref[...]
    h = (x * (1.0 + sc_ref[...]) + sh_ref[...]).astype(BF16)
    for j in range(D_FF // tf):
        a = _dot(h, wu_ref[:, j * tf:(j + 1) * tf])
        b = _dot(h, wu_ref[:, D_FF + j * tf:D_FF + (j + 1) * tf])
        g = (a * jax.nn.sigmoid(a) * b).astype(BF16)
        d = _dot(g, wd_ref[j * tf:(j + 1) * tf, :])
        if j == 0:
            acc_ref[...] = d
        else:
            acc_ref[...] += d
    y = alpha * x + 0.5 * gt_ref[...] * acc_ref[...]
    o_ref[...] = _layer_norm(y, lg_ref[...], lb_ref[...])


def _ffn(x, mod, j0, w_up, w_down, ln_g, ln_b, alpha, tm):
    m = x.shape[0]
    tok = pl.BlockSpec((tm, D_MODEL), lambda i: (i, 0))
    return pl.pallas_call(
        functools.partial(_ffn_kernel, alpha=alpha, tf=256),
        grid=(m // tm,),
        in_specs=[tok, mod.spec(j0, tm), mod.spec(j0 + 1, tm), mod.spec(j0 + 2, tm),
                  _const_spec((D_MODEL, 2 * D_FF)), _const_spec((D_FF, D_MODEL)),
                  _const_spec((1, D_MODEL)), _const_spec((1, D_MODEL))],
        out_specs=tok,
        out_shape=jax.ShapeDtypeStruct((m, D_MODEL), F32),
        scratch_shapes=[pltpu.VMEM((tm, D_MODEL), F32)],
        compiler_params=_cparams(("arbitrary",)),
        name="ffn",
    )(x, mod.arr, mod.arr, mod.arr, w_up, w_down, ln_g.reshape(1, -1), ln_b.reshape(1, -1))


def _inproj_kernel(x_ref, sh_ref, sc_ref, cos_ref, sin_ref, win_ref, wgate_ref, bgate_ref,
                   gq_ref_, wuq_ref, wuk_ref, gkv_ref_,
                   gq_o, gk_o, gv_o, gr_o, la_o, sbq_o, sbk_o, sbv_o, sbk16_o, sbv16_o,
                   qcat_o, ckv_o, kr_o, kcat_o):
    x = x_ref[...]
    h = (x * (1.0 + sc_ref[...]) + sh_ref[...]).astype(BF16)
    cos, sin = cos_ref[...], sin_ref[...]

    def proj(lo, hi):
        return _dot(h, win_ref[:, lo:hi])

    gq_o[...] = proj(C_GQ, C_GK) * (GLA_DK ** -0.5)
    gk_o[...] = proj(C_GK, C_GV)
    gv_o[...] = proj(C_GV, C_GR)
    gr_o[...] = proj(C_GR, C_SQ)
    gg = proj(C_GG, N_INRE).astype(BF16)
    gate = _dot(gg, wgate_ref[...]) + bgate_ref[...]
    log_sig = jnp.minimum(gate, 0.0) - jnp.log(1.0 + jnp.exp(-jnp.abs(gate)))
    la_o[...] = log_sig * (1.0 / GLA_TAU)

    sbq_o[...] = (proj(C_SQ, C_SK) * SB_SCALE).astype(BF16)
    sk = proj(C_SK, C_SV)
    sv = proj(C_SV, C_DQ)
    sbk_o[...] = sk
    sbv_o[...] = sv
    sbk16_o[...] = sk.astype(BF16)
    sbv16_o[...] = sv.astype(BF16)

    cq = _rms(proj(C_DQ, C_DKV), gq_ref_[...]).astype(BF16)
    q_nope = _dot(cq, wuq_ref[:, 0:512]).astype(BF16)
    q_lat = _dot(q_nope, wuk_ref[...]) * MLA_SCALE
    q_r = _dot(cq, wuq_ref[:, 512:1536])
    q_rs = _dot(cq, wuq_ref[:, 1536:2560])
    for hd in range(MLA_HEADS):
        sl = slice(hd * LANES, (hd + 1) * LANES)
        qcat_o[:, 2 * hd * LANES:(2 * hd + 1) * LANES] = q_lat[:, sl].astype(BF16)
        rot = (q_r[:, sl] * cos + q_rs[:, sl] * sin) * MLA_SCALE
        qcat_o[:, (2 * hd + 1) * LANES:(2 * hd + 2) * LANES] = rot.astype(BF16)

    ckv = _rms(proj(C_DKV, C_KR), gkv_ref_[...])
    ckv_o[...] = ckv
    kr = proj(C_KR, C_KRS) * cos + proj(C_KRS, C_GG) * sin
    kr_o[...] = kr[:, 0:MLA_ROPE]
    kcat_o[:, 0:LANES] = ckv.astype(BF16)
    kcat_o[:, LANES:2 * LANES] = kr.astype(BF16)


def _inproj(x, mod, cos_t, sin_t, pos_spec, lw, tm):
    m = x.shape[0]

    def tok(w):
        return pl.BlockSpec((tm, w), lambda i: (i, 0))

    outs = [(256, F32), (256, F32), (512, F32), (512, F32), (256, F32),
            (1024, BF16), (128, F32), (128, F32), (128, BF16), (128, BF16),
            (2048, BF16), (128, F32), (32, F32), (256, BF16)]
    return pl.pallas_call(
        _inproj_kernel,
        grid=(m // tm,),
        in_specs=[tok(D_MODEL), mod.spec(3, tm), mod.spec(4, tm), pos_spec, pos_spec,
                  _const_spec((D_MODEL, N_INRE)), _const_spec((LANES, 256)), _const_spec((1, 256)),
                  _const_spec((1, MLA_Q_RANK)), _const_spec((MLA_Q_RANK, 2560)),
                  _const_spec((512, 1024)), _const_spec((1, MLA_KV_RANK))],
        out_specs=[tok(w) for w, _ in outs],
        out_shape=[jax.ShapeDtypeStruct((m, w), dt) for w, dt in outs],
        compiler_params=_cparams(("arbitrary",)),
        name="inproj",
    )(x, mod.arr, mod.arr, cos_t, sin_t, lw['w_in_re'], lw['w_gate_pad'], lw['b_gate'],
      lw['g_q_norm'], lw['w_uq_re'], lw['w_uk_bd'], lw['g_kv_norm'])


def _gla_kernel(q_ref, k_ref, v_ref, la_ref, s0_ref, o_ref, sout_ref, s_scr, *, chunk, n_chunks):
    i = pl.program_id(1)

    @pl.when(i == 0)
    def _():
        s_scr[...] = s0_ref[...]

    row = lax.broadcasted_iota(jnp.int32, (chunk, chunk), 0)
    col = lax.broadcasted_iota(jnp.int32, (chunk, chunk), 1)
    causal = col <= row
    tri = causal.astype(F32)
    ones = jnp.ones((chunk, GLA_DV), F32)
    mid = chunk // 2

    def body(c, carry):
        r0 = pl.multiple_of(c * chunk, chunk)
        q = q_ref[pl.ds(r0, chunk), :]
        k = k_ref[pl.ds(r0, chunk), :]
        v = v_ref[pl.ds(r0, chunk), :]
        la = la_ref[pl.ds(r0, chunk), :]
        b = jnp.dot(tri, la, preferred_element_type=F32, precision=lax.Precision.HIGHEST)
        b_mid = b[mid - 1:mid, :]
        b_last = b[chunk - 1:chunk, :]
        q_hat = (q * jnp.exp(b - b_mid)).astype(BF16)
        k_hat = (k * jnp.exp(b_mid - b)).astype(BF16)
        q_in = (q * jnp.exp(b)).astype(BF16)
        k_out = (k * jnp.exp(b_last - b)).astype(BF16)
        v16 = v.astype(BF16)
        for hd in range(GLA_HEADS):
            ks = slice(hd * GLA_DK, (hd + 1) * GLA_DK)
            vs = slice(hd * GLA_DV, (hd + 1) * GLA_DV)
            att = jnp.where(causal, _dot_nt(q_hat[:, ks], k_hat[:, ks]), 0.0)
            s = s_scr[hd]
            o = _dot(att.astype(BF16), v16[:, vs]) + _dot(q_in[:, ks], s.astype(BF16))
            o_ref[pl.ds(r0, chunk), vs] = o
            decay = jnp.exp(_dot_tn(la[:, ks], ones, precision=lax.Precision.HIGHEST))
            s_scr[hd] = decay * s + _dot_tn(k_out[:, ks], v16[:, vs])
        return carry

    lax.fori_loop(0, n_chunks, body, 0)

    @pl.when(i == pl.num_programs(1) - 1)
    def _():
        sout_ref[...] = s_scr[...]


def _gla(q, k, v, la, s0, seq_len, chunk, tt):
    m = q.shape[0]
    g = m // seq_len
    tps = seq_len // tt

    def tok(w):
        return pl.BlockSpec((tt, w), lambda b, i: (b * tps + i, 0))

    st = pl.BlockSpec((None, GLA_HEADS, GLA_DK, GLA_DV), lambda b, i: (b, 0, 0, 0))
    return pl.pallas_call(
        functools.partial(_gla_kernel, chunk=chunk, n_chunks=tt // chunk),
        grid=(g, tps),
        in_specs=[tok(256), tok(256), tok(512), tok(256), st],
        out_specs=[tok(512), st],
        out_shape=[jax.ShapeDtypeStruct((m, 512), F32),
                   jax.ShapeDtypeStruct((g, GLA_HEADS, GLA_DK, GLA_DV), F32)],
        scratch_shapes=[pltpu.VMEM((GLA_HEADS, GLA_DK, GLA_DV), F32)],
        compiler_params=_cparams(("arbitrary", "arbitrary")),
        name="gla",
    )(q, k, v, la, s0)


def _sb_block(z, mask, upper, r, v16):
    lk = -(jnp.maximum(z, 0.0) + jnp.log(1.0 + jnp.exp(-jnp.abs(z))))
    ls = z + lk
    if mask is not None:
        lk = jnp.where(mask, lk, 0.0)
    hi = lk.astype(BF16)
    lo = (lk - hi.astype(F32)).astype(BF16)
    between = _dot(hi, upper) + _dot(lo, upper)
    w = jnp.exp(ls + between + r)
    if mask is not None:
        w = jnp.where(mask, w, 0.0)
    return _dot(w.astype(BF16), v16), r + jnp.sum(lk, axis=-1, keepdims=True)


def _upper(tk):
    row = lax.broadcasted_iota(jnp.int32, (tk, tk), 0)
    col = lax.broadcasted_iota(jnp.int32, (tk, tk), 1)
    return (row > col).astype(BF16)


def _sb_prompt_kernel(q_ref, k_ref, v_ref, o_ref, *, tq):
    qi = pl.program_id(1)
    upper = _upper(tq)
    row = lax.broadcasted_iota(jnp.int32, (tq, tq), 0)
    col = lax.broadcasted_iota(jnp.int32, (tq, tq), 1)
    earlier = col < row
    for hd in range(SB_HEADS):
        qh = q_ref[:, hd * LANES:(hd + 1) * LANES]

        def block(kb, r, mask):
            r0 = pl.multiple_of(kb * tq, tq)
            kblk = k_ref[pl.ds(r0, tq), :]
            vblk = v_ref[pl.ds(r0, tq), :]
            return _sb_block(_dot_nt(qh, kblk), mask, upper, r, vblk)

        acc, r = block(qi, jnp.zeros((tq, 1), F32), earlier)

        def cond(c):
            return jnp.logical_and(c[0] >= 0, jnp.max(c[1]) > SB_DEAD)

        def body(c):
            pv, r_new = block(c[0], c[1], None)
            return c[0] - 1, r_new, c[2] + pv

        _, _, acc = lax.while_loop(cond, body, (qi - 1, r, acc))
        o_ref[:, hd * LANES:(hd + 1) * LANES] = acc.astype(BF16)


def _sb_prompt(q, k16, v16, seq_len, tq):
    m = q.shape[0]
    g = m // seq_len
    tps = seq_len // tq
    qs = pl.BlockSpec((tq, SB_HEADS * LANES), lambda b, i: (b * tps + i, 0))
    ks = pl.BlockSpec((seq_len, LANES), lambda b, i: (b, 0))
    return pl.pallas_call(
        functools.partial(_sb_prompt_kernel, tq=tq),
        grid=(g, tps),
        in_specs=[qs, ks, ks],
        out_specs=qs,
        out_shape=jax.ShapeDtypeStruct((m, SB_HEADS * LANES), BF16),
        compiler_params=_cparams(("arbitrary", "arbitrary")),
        name="sb_prompt",
    )(q, k16, v16)


def _mla_prompt_kernel(q_ref, k_ref, o_ref, *, tq):
    qi = pl.program_id(1)
    row = lax.broadcasted_iota(jnp.int32, (tq, tq), 0)
    col = lax.broadcasted_iota(jnp.int32, (tq, tq), 1)
    visible = col <= row
    for hd in range(MLA_HEADS):
        qh = q_ref[:, 2 * hd * LANES:(2 * hd + 2) * LANES]

        def block(kb, m_i, l_i, acc, mask):
            r0 = pl.multiple_of(kb * tq, tq)
            kblk = k_ref[pl.ds(r0, tq), :]
            s = _dot_nt(qh, kblk)
            if mask is not None:
                s = jnp.where(mask, s, NEG_BIG)
            m_new = jnp.maximum(m_i, jnp.max(s, axis=-1, keepdims=True))
            p = jnp.exp(s - m_new)
            corr = jnp.exp(m_i - m_new)
            l_new = l_i * corr + jnp.sum(p, axis=-1, keepdims=True)
            acc_new = acc * corr + _dot(p.astype(BF16), kblk[:, 0:MLA_KV_RANK])
            return m_new, l_new, acc_new

        init = (jnp.full((tq, 1), NEG_BIG, F32), jnp.zeros((tq, 1), F32),
                jnp.zeros((tq, MLA_KV_RANK), F32))
        m_i, l_i, acc = block(qi, *init, visible)
        m_i, l_i, acc = lax.fori_loop(0, qi, lambda kb, c: block(kb, *c, None), (m_i, l_i, acc))
        o_ref[:, hd * LANES:(hd + 1) * LANES] = (acc / l_i).astype(BF16)


def _mla_prompt(qcat, kcat, seq_len, tq):
    m = qcat.shape[0]
    g = m // seq_len
    tps = seq_len // tq
    return pl.pallas_call(
        functools.partial(_mla_prompt_kernel, tq=tq),
        grid=(g, tps),
        in_specs=[pl.BlockSpec((tq, 2 * MLA_HEADS * LANES), lambda b, i: (b * tps + i, 0)),
                  pl.BlockSpec((seq_len, 2 * LANES), lambda b, i: (b, 0))],
        out_specs=pl.BlockSpec((tq, MLA_HEADS * LANES), lambda b, i: (b * tps + i, 0)),
        out_shape=jax.ShapeDtypeStruct((m, MLA_HEADS * LANES), BF16),
        compiler_params=_cparams(("arbitrary", "arbitrary")),
        name="mla_prompt",
    )(qcat, kcat)


def _sample_attn_kernel(pt_ref, sbq_ref, sbkn_ref, sbvn_ref, ql_ref, qr_ref, ckvn_ref, krn_ref,
                        *rest, n_pp, dec_seq):
    sbk_refs = rest[0:n_pp]
    sbv_refs = rest[n_pp:2 * n_pp]
    ckv_refs = rest[2 * n_pp:3 * n_pp]
    kr_refs = rest[3 * n_pp:4 * n_pp]
    osb_ref, omla_ref, r_scr, asb_scr, m_scr, l_scr, amla_scr = rest[4 * n_pp:]
    j = pl.program_id(1)
    rows = sbq_ref.shape[0]
    upper = _upper(PAGE)
    sbq = sbq_ref[...]
    ql = ql_ref[...]
    qr = qr_ref[...]

    def mla_update(s, ckv16):
        m_i = m_scr[...]
        m_new = jnp.maximum(m_i, jnp.max(s, axis=-1, keepdims=True))
        p = jnp.exp(s - m_new)
        corr = jnp.exp(m_i - m_new)
        l_scr[...] = l_scr[...] * corr + jnp.sum(p, axis=-1, keepdims=True)
        amla_scr[...] = amla_scr[...] * corr + _dot(p.astype(BF16), ckv16)
        m_scr[...] = m_new

    @pl.when(j == 0)
    def _():
        tok = lax.broadcasted_iota(jnp.int32, (rows, PAGE), 0) % dec_seq
        key = lax.broadcasted_iota(jnp.int32, (rows, PAGE), 1)
        kn = sbkn_ref[...]
        vn = sbvn_ref[...]
        pv, r = _sb_block(_dot_nt(sbq, kn), key < tok, upper, jnp.zeros((rows, 1), F32), vn)
        asb_scr[...] = pv
        r_scr[...] = r
        cn = ckvn_ref[...]
        rn = krn_ref[...]
        s = jnp.where(key <= tok, _dot_nt(ql, cn) + _dot_nt(qr, rn), NEG_BIG)
        m_scr[...] = jnp.full((rows, 1), NEG_BIG, F32)
        l_scr[...] = jnp.zeros((rows, 1), F32)
        amla_scr[...] = jnp.zeros((rows, LANES), F32)
        mla_update(s, cn)

    for p in range(n_pp):
        pv, r = _sb_block(_dot_nt(sbq, sbk_refs[p][...].astype(BF16)), None, upper, r_scr[...],
                          sbv_refs[p][...].astype(BF16))
        asb_scr[...] += pv
        r_scr[...] = r
        ckv16 = ckv_refs[p][...].astype(BF16)
        kr16 = kr_refs[p][...].astype(BF16)
        mla_update(_dot_nt(ql, ckv16) + _dot_nt(qr[:, 0:MLA_ROPE], kr16), ckv16)

    @pl.when(j == pl.num_programs(1) - 1)
    def _():
        osb_ref[...] = asb_scr[...]
        omla_ref[...] = amla_scr[...] / l_scr[...]


def _sample_attn(page_table, layer, caches, sbq, sbkn, sbvn, ql, qr, ckvn, krn, dec_seq, n_pp):
    nb, n_pages = page_table.shape
    rows = sbq.shape[1]
    steps = n_pages // n_pp
    pt = page_table.reshape(-1)
    c_sbk, c_sbv, c_ckv, c_kr = caches

    def seq(r, w):
        return pl.BlockSpec((None, r, w), lambda b, j, pt_: (b, 0, 0))

    def page(p, w):
        def imap(b, j, pt_):
            return (pt_[b * n_pages + (n_pages - 1 - (j * n_pp + p))], layer, 0, 0)
        return pl.BlockSpec((None, None, PAGE, w), imap)

    nk = sbkn.shape[1]
    in_specs = [seq(rows, LANES), seq(nk, LANES), seq(nk, LANES), seq(rows, LANES), seq(rows, LANES),
                seq(nk, LANES), seq(nk, LANES)]
    for w in (LANES, LANES, LANES, MLA_ROPE):
        in_specs += [page(p, w) for p in range(n_pp)]
    grid_spec = pltpu.PrefetchScalarGridSpec(
        num_scalar_prefetch=1,
        grid=(nb, steps),
        in_specs=in_specs,
        out_specs=[seq(rows, LANES), seq(rows, LANES)],
        scratch_shapes=[pltpu.VMEM((rows, 1), F32), pltpu.VMEM((rows, LANES), F32),
                        pltpu.VMEM((rows, 1), F32), pltpu.VMEM((rows, 1), F32),
                        pltpu.VMEM((rows, LANES), F32)],
    )
    return pl.pallas_call(
        functools.partial(_sample_attn_kernel, n_pp=n_pp, dec_seq=dec_seq),
        grid_spec=grid_spec,
        out_shape=[jax.ShapeDtypeStruct((nb, rows, LANES), F32)] * 2,
        compiler_params=_cparams(("arbitrary", "arbitrary")),
        name="sample_attn",
    )(pt, sbq, sbkn, sbvn, ql, qr, ckvn, krn,
      *([c_sbk] * n_pp), *([c_sbv] * n_pp), *([c_ckv] * n_pp), *([c_kr] * n_pp))


def _merge_kernel(x_ref, sh_ref, sc_ref, gt_ref, ogla_ref, glar_ref, osb_ref, olat_ref,
                  wg_ref, ggn_ref, wbg_ref, wbs_ref, wuv_ref, wbm_ref, wo_ref, lg_ref, lb_ref,
                  o_ref, *, alpha):
    x = x_ref[...]
    h = (x * (1.0 + sc_ref[...]) + sh_ref[...]).astype(BF16)
    ogla = ogla_ref[...]
    ggn = ggn_ref[...]
    normed = []
    for hd in range(GLA_HEADS):
        sl = slice(hd * GLA_DV, (hd + 1) * GLA_DV)
        normed.append(_rms(ogla[:, sl], ggn[:, sl]))
    r = glar_ref[...]
    og = (jnp.concatenate(normed, axis=-1) * (r * jax.nn.sigmoid(r))).astype(BF16)
    merged = jax.nn.sigmoid(_dot(h, wg_ref[:, 0:D_MODEL])) * _dot(og, wbg_ref[...])
    merged += jax.nn.sigmoid(_dot(h, wg_ref[:, D_MODEL:2 * D_MODEL])) * _dot(osb_ref[...], wbs_ref[...])
    v_mla = _dot(olat_ref[...], wuv_ref[...]).astype(BF16)
    merged += jax.nn.sigmoid(_dot(h, wg_ref[:, 2 * D_MODEL:3 * D_MODEL])) * _dot(v_mla, wbm_ref[...])
    mix = _dot(merged.astype(BF16), wo_ref[...])
    y = alpha * x + gt_ref[...] * mix
    o_ref[...] = _layer_norm(y, lg_ref[...], lb_ref[...])


def _merge(x, mod, o_gla, gla_r, o_sb, o_lat, lw, ln_g, ln_b, alpha, tm):
    m = x.shape[0]

    def tok(w):
        return pl.BlockSpec((tm, w), lambda i: (i, 0))

    return pl.pallas_call(
        functools.partial(_merge_kernel, alpha=alpha),
        grid=(m // tm,),
        in_specs=[tok(D_MODEL), mod.spec(3, tm), mod.spec(4, tm), mod.spec(5, tm),
                  tok(512), tok(512), tok(1024), tok(1024),
                  _const_spec((D_MODEL, 3 * D_MODEL)), _const_spec((1, 512)),
                  _const_spec((512, D_MODEL)), _const_spec((1024, D_MODEL)),
                  _const_spec((1024, 512)), _const_spec((512, D_MODEL)),
                  _const_spec((D_MODEL, D_MODEL)), _const_spec((1, D_MODEL)), _const_spec((1, D_MODEL))],
        out_specs=tok(D_MODEL),
        out_shape=jax.ShapeDtypeStruct((m, D_MODEL), F32),
        compiler_params=_cparams(("arbitrary",)),
        name="merge",
    )(x, mod.arr, mod.arr, mod.arr, o_gla, gla_r, o_sb, o_lat,
      lw['w_gates'], lw['g_gla_norm'], lw['w_br_gla'], lw['w_br_sb_pad'], lw['w_uv_bd'],
      lw['w_br_mla'], lw['w_out'], ln_g.reshape(1, -1), ln_b.reshape(1, -1))


def _layer_weights(l, w_in, w_gla_gate, b_gla_gate, g_gla_norm, g_q_norm, w_uq, g_kv_norm, w_uk,
                   w_uv, w_br_gla, w_br_sb, w_br_mla, w_out):
    win = w_in[l]
    offs = np.cumsum((256, 256, 512, 512, 16, 512, 128, 128, 256, 128, 32, 3 * D_MODEL))
    gq, gk, gv, gr, gg, sq, sk, sv, dq, dkv, dkr, gates = jnp.split(win, offs[:-1].tolist(), axis=1)
    zeros = lambda n: jnp.zeros((D_MODEL, n), F32)
    sq_heads = []
    for hd in range(SB_HEADS):
        w = sq[:, hd * SB_HEAD_DIM:(hd + 1) * SB_HEAD_DIM]
        sq_heads += [w, zeros(64)] if hd < SB_HEADS // SB_KV_HEADS else [zeros(64), w]
    half = MLA_ROPE // 2
    dkr_sw = jnp.concatenate([dkr[:, half:], dkr[:, :half]], axis=1)
    w_in_re = jnp.concatenate(
        [gq, gk, gv, gr] + sq_heads + [sk, sv, dq, dkv, dkr, zeros(96), dkr_sw, zeros(96), gg, zeros(112)],
        axis=1).astype(BF16)
    w_gate_pad = jnp.concatenate([w_gla_gate[l], jnp.zeros((LANES - GLA_GATE_RANK, 256), F32)],
                                 axis=0).astype(BF16)
    uq = w_uq[l].reshape(MLA_Q_RANK, MLA_HEADS, MLA_NOPE + MLA_ROPE)
    nope = uq[:, :, :MLA_NOPE].reshape(MLA_Q_RANK, -1)
    rope_w = uq[:, :, MLA_NOPE:]
    rope_sw = jnp.concatenate([rope_w[:, :, half:], rope_w[:, :, :half]], axis=-1)
    padr = lambda t: jnp.pad(t, ((0, 0), (0, 0), (0, LANES - MLA_ROPE))).reshape(MLA_Q_RANK, -1)
    w_uq_re = jnp.concatenate([nope, padr(rope_w), padr(rope_sw)], axis=1).astype(BF16)
    uk = w_uk[l].reshape(MLA_KV_RANK, MLA_HEADS, MLA_NOPE)
    uv = w_uv[l].reshape(MLA_KV_RANK, MLA_HEADS, MLA_V)
    eye = jnp.eye(MLA_HEADS, dtype=F32)
    w_uk_bd = jnp.einsum('rhn,hg->hngr', uk, eye).reshape(MLA_HEADS * MLA_NOPE, MLA_HEADS * MLA_KV_RANK)
    w_uv_bd = jnp.einsum('rhv,hg->hrgv', uv, eye).reshape(MLA_HEADS * MLA_KV_RANK, MLA_HEADS * MLA_V)
    brs = w_br_sb[l].reshape(SB_HEADS, SB_HEAD_DIM, D_MODEL)
    zer = jnp.zeros((SB_HEAD_DIM, D_MODEL), F32)
    brs_rows = []
    for hd in range(SB_HEADS):
        brs_rows += [brs[hd], zer] if hd < SB_HEADS // SB_KV_HEADS else [zer, brs[hd]]
    return dict(
        w_in_re=w_in_re, w_gate_pad=w_gate_pad, b_gate=b_gla_gate[l].reshape(1, -1),
        g_q_norm=g_q_norm[l].reshape(1, -1), w_uq_re=w_uq_re, w_uk_bd=w_uk_bd.astype(BF16),
        g_kv_norm=g_kv_norm[l].reshape(1, -1), w_gates=gates.astype(BF16),
        g_gla_norm=g_gla_norm[l].reshape(1, -1), w_br_gla=w_br_gla[l].astype(BF16),
        w_br_sb_pad=jnp.concatenate(brs_rows, axis=0).astype(BF16), w_uv_bd=w_uv_bd.astype(BF16),
        w_br_mla=w_br_mla[l].astype(BF16), w_out=w_out[l].astype(BF16))


def _rope_tables(pos):
    half = MLA_ROPE // 2
    inv_freq = ROPE_THETA ** (-jnp.arange(half, dtype=F32) / half)
    ang = pos.astype(F32)[:, None] * inv_freq[None, :]
    cos, sin = jnp.cos(ang), jnp.sin(ang)
    pad = ((0, 0), (0, LANES - MLA_ROPE))
    return (jnp.pad(jnp.concatenate([cos, cos], axis=1), pad),
            jnp.pad(jnp.concatenate([-sin, sin], axis=1), pad))


def _heads_to_rows(t, n_seq, dec_seq, n_heads):
    return t.reshape(n_seq, dec_seq, n_heads, LANES).transpose(0, 2, 1, 3).reshape(n_seq, n_heads * dec_seq, LANES)


def _rows_to_heads(t, n_seq, dec_seq, n_heads):
    return t.reshape(n_seq, n_heads, dec_seq, LANES).transpose(0, 2, 1, 3).reshape(n_seq * dec_seq, n_heads * LANES)


def kernel(x_prompt, x_sample, cache_sb_k, cache_sb_v, cache_mla_ckv, cache_mla_kr, state_gla,
           page_table, c_prompt, c_sample, w_in, w_gla_gate, b_gla_gate, g_gla_norm, g_q_norm, w_uq,
           g_kv_norm, w_uk, w_uv, w_br_gla, w_br_sb, w_br_mla, w_out, w_mod, b_mod, ln_g, ln_b,
           w_ffn_up, w_ffn_down):
    nb, seq, _ = x_prompt.shape
    ns, dec_seq, _ = x_sample.shape
    depth = w_in.shape[0]
    n_pool, n_pages = cache_sb_k.shape[0], page_table.shape[1]
    past_len = n_pages * PAGE
    alpha = (2 * depth) ** 0.25

    mp, ms = nb * seq, ns * dec_seq
    tm_p = min(512, seq)
    tm_s = min(512, ms)
    tq = min(256, seq)
    gla_chunk = min(64, seq)
    gla_tile = min(512, seq)
    dec_pad = 8
    n_pp = min(8, n_pages)

    xp = x_prompt.reshape(mp, D_MODEL)
    xs = x_sample.reshape(ms, D_MODEL)
    c_p = jnp.pad(c_prompt, ((0, (-nb) % 8), (0, 0)))
    c_s = jnp.repeat(c_sample, dec_seq, axis=0)
    cos_p, sin_p = _rope_tables(jnp.arange(seq))
    cos_s, sin_s = _rope_tables(jnp.tile(past_len + jnp.arange(dec_seq), ns))
    pos_spec_p = pl.BlockSpec((tm_p, LANES), lambda i: (i % (seq // tm_p), 0))
    pos_spec_s = pl.BlockSpec((tm_s, LANES), lambda i: (i, 0))
    caches = (cache_sb_k.reshape(n_pool, depth, PAGE, LANES), cache_sb_v.reshape(n_pool, depth, PAGE, LANES),
              cache_mla_ckv, cache_mla_kr)
    zero_state = jnp.zeros((nb, GLA_HEADS, GLA_DK, GLA_DV), F32)

    states_p, states_s = [], []
    for l in range(depth):
        lw = _layer_weights(l, w_in, w_gla_gate, b_gla_gate, g_gla_norm, g_q_norm, w_uq, g_kv_norm,
                            w_uk, w_uv, w_br_gla, w_br_sb, w_br_mla, w_out)
        w_mod16 = w_mod[l].astype(BF16)
        w_up16 = w_ffn_up[l].astype(BF16)
        w_dn16 = w_ffn_down[l].astype(BF16)
        mod_p = _Mod(_modulation(c_p, w_mod16, b_mod[l]), False, seq // tm_p)
        mod_s = _Mod(_modulation(c_s, w_mod16, b_mod[l]), True, None)

        xp = _ffn(xp, mod_p, 0, w_up16[0], w_dn16[0], ln_g[l, 0], ln_b[l, 0], alpha, tm_p)
        (gq, gk, gv, gr, la, sbq, sbk, sbv, sbk16, sbv16, qcat, ckv, kr, kcat) = _inproj(
            xp, mod_p, cos_p, sin_p, pos_spec_p, lw, tm_p)
        o_gla, s_gla = _gla(gq, gk, gv, la, zero_state, seq, gla_chunk, gla_tile)
        o_sb = _sb_prompt(sbq, sbk16, sbv16, seq, tq)
        o_lat = _mla_prompt(qcat, kcat, seq, tq)
        xp = _merge(xp, mod_p, o_gla, gr, o_sb, o_lat, lw, ln_g[l, 1], ln_b[l, 1], alpha, tm_p)
        xp = _ffn(xp, mod_p, 6, w_up16[1], w_dn16[1], ln_g[l, 2], ln_b[l, 2], alpha, tm_p)
        states_p.append((sbk.reshape(nb, seq, SB_KV_HEADS, SB_HEAD_DIM),
                         sbv.reshape(nb, seq, SB_KV_HEADS, SB_HEAD_DIM),
                         ckv.reshape(nb, seq, MLA_KV_RANK), kr.reshape(nb, seq, MLA_ROPE), s_gla))

        xs = _ffn(xs, mod_s, 0, w_up16[0], w_dn16[0], ln_g[l, 0], ln_b[l, 0], alpha, tm_s)
        (gq, gk, gv, gr, la, sbq, sbk, sbv, sbk16, sbv16, qcat, ckv, kr, kcat) = _inproj(
            xs, mod_s, cos_s, sin_s, pos_spec_s, lw, tm_s)

        def pad_seq(t):
            t = t.reshape(ns, dec_seq, t.shape[-1])
            return jnp.pad(t, ((0, 0), (0, dec_pad - dec_seq), (0, 0)))

        def pad_page(t):
            t = t.reshape(ns, dec_seq, t.shape[-1])
            return jnp.pad(t, ((0, 0), (0, PAGE - dec_seq), (0, 0)))

        o_gla, s_gla = _gla(*(pad_seq(t).reshape(ns * dec_pad, -1) for t in (gq, gk, gv, la)),
                            state_gla[:, l], dec_pad, dec_pad, dec_pad)
        o_gla = o_gla.reshape(ns, dec_pad, -1)[:, :dec_seq].reshape(ms, -1)
        qc = qcat.reshape(ms, MLA_HEADS, 2, LANES)
        o_sb, o_lat = _sample_attn(
            page_table, l, caches,
            _heads_to_rows(sbq, ns, dec_seq, SB_HEADS),
            pad_page(sbk16), pad_page(sbv16),
            _heads_to_rows(qc[:, :, 0].reshape(ms, -1), ns, dec_seq, MLA_HEADS),
            _heads_to_rows(qc[:, :, 1].reshape(ms, -1), ns, dec_seq, MLA_HEADS),
            pad_page(kcat[:, 0:LANES]), pad_page(kcat[:, LANES:]), dec_seq, n_pp)
        o_sb = _rows_to_heads(o_sb, ns, dec_seq, SB_HEADS).astype(BF16)
        o_lat = _rows_to_heads(o_lat, ns, dec_seq, MLA_HEADS).astype(BF16)
        xs = _merge(xs, mod_s, o_gla, gr, o_sb, o_lat, lw, ln_g[l, 1], ln_b[l, 1], alpha, tm_s)
        xs = _ffn(xs, mod_s, 6, w_up16[1], w_dn16[1], ln_g[l, 2], ln_b[l, 2], alpha, tm_s)
        states_s.append((sbk.reshape(ns, dec_seq, SB_KV_HEADS, SB_HEAD_DIM),
                         sbv.reshape(ns, dec_seq, SB_KV_HEADS, SB_HEAD_DIM),
                         ckv.reshape(ns, dec_seq, MLA_KV_RANK), kr.reshape(ns, dec_seq, MLA_ROPE), s_gla))

    outs_p = tuple(jnp.stack([s[i] for s in states_p], axis=1) for i in range(5))
    outs_s = tuple(jnp.stack([s[i] for s in states_s], axis=1) for i in range(5))
    return (xp.reshape(nb, seq, D_MODEL), xs.reshape(ns, dec_seq, D_MODEL)) + outs_p + outs_s
```
